```python
import jax, jax.numpy as jnp
from jax import lax
import numpy as np

D_MODEL = 2048
BATCH = 4
SEQ = 2048
DEPTH = 2

CHUNK = 64
N_MEM = 256
SB_HEAD_DIM = 128
SB_WIDTH = D_MODEL // 2
SB_HEADS = SB_WIDTH // SB_HEAD_DIM
SB_QBLOCK = 128
GM_GROUP_DIM = 128
GM_WIDTH = D_MODEL // 2
GM_GROUPS = GM_WIDTH // GM_GROUP_DIM
GM_BLOCK = 128
XA_HEADS = 4
XA_WIDTH = D_MODEL // 2
XA_HEAD_DIM = XA_WIDTH // XA_HEADS
N_BRANCH = 3
IN_WIDTH = 3 * SB_WIDTH + 2 * GM_WIDTH + XA_WIDTH
D_FF = 5632
CONV_W = 3
EPS = 1e-6

kernel_name = "hybrid_stickbreak_gmlp_memxattn_convffn"


def rmsnorm(x, g):
    xf = x.astype(jnp.float32)
    xf = xf * lax.rsqrt(jnp.mean(xf * xf, axis=-1, keepdims=True) + EPS)
    return xf.astype(x.dtype) * g


def stick_breaking_attention(q, k, v):
    B, S, H, Dh = q.shape
    scale = Dh ** -0.5
    outs = []
    for i in range(S // SB_QBLOCK):
        q0 = i * SB_QBLOCK
        q1 = q0 + SB_QBLOCK
        qb = q[:, q0:q1]
        kb = k[:, :q1]
        vb = v[:, :q1]
        z = jnp.einsum('bthd,bshd->bhts', qb, kb).astype(jnp.float32) * scale
        t_pos = q0 + jnp.arange(SB_QBLOCK)[:, None]
        s_pos = jnp.arange(q1)[None, :]
        strict = s_pos < t_pos
        log_keep = jnp.where(strict, jax.nn.log_sigmoid(-z), 0.0)
        suffix = lax.cumsum(log_keep, axis=3, reverse=True) - log_keep
        log_a = jax.nn.log_sigmoid(z) + suffix
        a = jnp.where(strict, jnp.exp(log_a), 0.0)
        outs.append(jnp.einsum('bhts,bshd->bthd', a.astype(v.dtype), vb))
    return jnp.concatenate(outs, axis=1)


def spatial_gating(u, v, g_vnorm, w_s, b_s):
    B, S, _ = u.shape
    u = jax.nn.gelu(u)
    v = rmsnorm(jax.nn.gelu(v), g_vnorm)
    pos = jnp.arange(GM_BLOCK)
    mask = (pos[None, :] // CHUNK) <= (pos[:, None] // CHUNK)
    w = jnp.where(mask[None], w_s, jnp.zeros_like(w_s))
    vb = v.reshape(B, S // GM_BLOCK, GM_BLOCK, GM_GROUPS, GM_GROUP_DIM)
    mixed = jnp.einsum('gts,bcsgd->bctgd', w, vb) + b_s.T[None, None, :, :, None]
    return u * mixed.reshape(B, S, GM_WIDTH)


def memory_cross_attention(q, mem_kv):
    B, M, _ = mem_kv.shape
    k, v = jnp.split(mem_kv, 2, axis=-1)
    k = k.reshape(B, M, XA_HEADS, XA_HEAD_DIM)
    v = v.reshape(B, M, XA_HEADS, XA_HEAD_DIM)
    z = jnp.einsum('bthd,bmhd->bhtm', q, k).astype(jnp.float32) * (XA_HEAD_DIM ** -0.5)
    p = jax.nn.softmax(z, axis=-1)
    return jnp.einsum('bhtm,bmhd->bthd', p.astype(v.dtype), v)


def conv_ffn(h, w_up, conv_w, conv_b, w_down):
    S = h.shape[1]
    up = h @ w_up
    gate, val = jnp.split(up, 2, axis=-1)
    gp = jnp.pad(gate, ((0, 0), (CONV_W - 1, 0), (0, 0)))
    conv = conv_b + sum(conv_w[i] * gp[:, i:i + S] for i in range(CONV_W))
    return (jax.nn.gelu(conv) * val) @ w_down


def setup_inputs(seed: int = 0) -> dict:
    key = jax.random.key(seed)
    ks = jax.random.split(key, 24)
    f32 = jnp.float32

    def nrm(k, shape, fan_in):
        return jax.random.normal(k, shape, f32) * (fan_in ** -0.5)

    def gain(k, n):
        return 1.0 + 0.05 * jax.random.normal(k, (DEPTH, n), f32)

    L = DEPTH
    return {
        "x": jax.random.normal(ks[0], (BATCH, SEQ, D_MODEL), f32),
        "mem": jax.random.normal(ks[1], (BATCH, N_MEM, D_MODEL), f32),
        "g_mix_pre": gain(ks[2], D_MODEL),
        "w_in": nrm(ks[3], (L, D_MODEL, IN_WIDTH), D_MODEL),
        "g_vnorm": gain(ks[4], GM_WIDTH),
        "w_s": nrm(ks[5], (L, GM_GROUPS, GM_BLOCK, GM_BLOCK), GM_BLOCK),
        "b_s": 1.0 + 0.01 * jax.random.normal(ks[6], (L, GM_GROUPS, GM_BLOCK), f32),
        "g_mem": gain(ks[7], D_MODEL),
        "w_mem_kv": nrm(ks[8], (L, D_MODEL, 2 * XA_WIDTH), D_MODEL),
        "w_gate": nrm(ks[9], (L, D_MODEL, N_BRANCH * D_MODEL), D_MODEL),
        "b_gate": 0.01 * jax.random.normal(ks[10], (L, N_BRANCH * D_MODEL), f32),
        "w_br_sb": nrm(ks[11], (L, SB_WIDTH, D_MODEL), SB_WIDTH),
        "w_br_gm": nrm(ks[12], (L, GM_WIDTH, D_MODEL), GM_WIDTH),
        "w_br_xa": nrm(ks[13], (L, XA_WIDTH, D_MODEL), XA_WIDTH),
        "w_out": nrm(ks[14], (L, D_MODEL, D_MODEL), D_MODEL),
        "g_mix_post": gain(ks[15], D_MODEL),
        "g_ffn_pre": gain(ks[16], D_MODEL),
        "w_up": nrm(ks[17], (L, D_MODEL, 2 * D_FF), D_MODEL),
        "conv_w": nrm(ks[18], (L, CONV_W, D_FF), CONV_W),
        "conv_b": 0.01 * jax.random.normal(ks[19], (L, D_FF), f32),
        "w_down": nrm(ks[20], (L, D_FF, D_MODEL), D_FF),
        "g_ffn_post": gain(ks[21], D_MODEL),
    }


def reference(x, mem, g_mix_pre, w_in, g_vnorm, w_s, b_s, g_mem, w_mem_kv, w_gate, b_gate,
              w_br_sb, w_br_gm, w_br_xa, w_out, g_mix_post, g_ffn_pre, w_up, conv_w, conv_b,
              w_down, g_ffn_post):
    B, S, D = x.shape
    splits = [SB_WIDTH, 2 * SB_WIDTH, 3 * SB_WIDTH,
              3 * SB_WIDTH + GM_WIDTH, 3 * SB_WIDTH + 2 * GM_WIDTH]
    for l in range(DEPTH):
        h = rmsnorm(x, g_mix_pre[l])
        proj = h @ w_in[l]
        q_sb, k_sb, v_sb, u_gm, v_gm, q_xa = jnp.split(proj, splits, axis=-1)
        hs = (B, S, SB_HEADS, SB_HEAD_DIM)
        o_sb = stick_breaking_attention(q_sb.reshape(hs), k_sb.reshape(hs),
                                        v_sb.reshape(hs)).reshape(B, S, SB_WIDTH)
        o_gm = spatial_gating(u_gm, v_gm, g_vnorm[l], w_s[l], b_s[l])
        mem_kv = rmsnorm(mem, g_mem[l]) @ w_mem_kv[l]
        o_xa = memory_cross_attention(q_xa.reshape(B, S, XA_HEADS, XA_HEAD_DIM),
                                      mem_kv).reshape(B, S, XA_WIDTH)
        gates = jax.nn.sigmoid(h @ w_gate[l] + b_gate[l]).reshape(B, S, N_BRANCH, D)
        merged = (gates[:, :, 0] * (o_sb @ w_br_sb[l])
                  + gates[:, :, 1] * (o_gm @ w_br_gm[l])
                  + gates[:, :, 2] * (o_xa @ w_br_xa[l]))
        x = x + rmsnorm(merged @ w_out[l], g_mix_post[l])
        h = rmsnorm(x, g_ffn_pre[l])
        x = x + rmsnorm(conv_ffn(h, w_up[l], conv_w[l], conv_b[l], w_down[l]), g_ffn_post[l])
    return x
```

```python
import functools

import jax
import jax.numpy as jnp
from jax import lax
from jax.experimental import pallas as pl
from jax.experimental.pallas import tpu as pltpu

F32 = jnp.float32
BF16 = jnp.bfloat16

EPS = 1e-6
CHUNK = 64
N_MEM = 256
SB_HEADS = 8
SB_HEAD_DIM = 128
GM_GROUPS = 8
GM_GROUP_DIM = 128
GM_BLOCK = 128
XA_HEADS = 4
XA_HEAD_DIM = 256
CONV_W = 3

V7X_VMEM_BYTES = 64 * 1024 * 1024
VMEM_LIMIT_BYTES = V7X_VMEM_BYTES - 8 * 1024 * 1024
F32_SUBLANES = 8


def _params(*semantics):
    return pltpu.CompilerParams(dimension_semantics=semantics,
                                vmem_limit_bytes=VMEM_LIMIT_BYTES)


def _rms(x):
    return x * lax.rsqrt(jnp.mean(x * x, axis=-1, keepdims=True) + EPS)


def _dot(a, b):
    return jnp.dot(a, b, preferred_element_type=F32)


def _dot_nt(a, b):
    return lax.dot_general(a, b, (((1,), (1,)), ((), ())), preferred_element_type=F32)


def _rmsnorm_kernel(x_ref, g_ref, o_ref):
    o_ref[...] = (_rms(x_ref[...]) * g_ref[...]).astype(o_ref.dtype)


def rmsnorm_bf16(x, g, *, tm=512):
    m, d = x.shape
    return pl.pallas_call(
        _rmsnorm_kernel,
        grid=(m // tm,),
        in_specs=[pl.BlockSpec((tm, d), lambda i: (i, 0)),
                  pl.BlockSpec((1, d), lambda i: (0, 0))],
        out_specs=pl.BlockSpec((tm, d), lambda i: (i, 0)),
        out_shape=jax.ShapeDtypeStruct((m, d), BF16),
        compiler_params=_params("arbitrary"),
        name="rmsnorm",
    )(x, g.reshape(1, d))


def _cast_kernel(w_ref, o_ref):
    o_ref[...] = w_ref[...].astype(o_ref.dtype)


def cast_bf16(w, layer, *, tk=512):
    _, k, n = w.shape
    return pl.pallas_call(
        _cast_kernel,
        grid=(k // tk,),
        in_specs=[pl.BlockSpec((None, tk, n), lambda i: (layer, i, 0))],
        out_specs=pl.BlockSpec((tk, n), lambda i: (i, 0)),
        out_shape=jax.ShapeDtypeStruct((k, n), BF16),
        compiler_params=_params("arbitrary"),
        name="cast_bf16",
    )(w)


def _mm_kernel(a_ref, w_ref, o_ref, wbf_ref):
    @pl.when(pl.program_id(1) == 0)
    def _():
        wbf_ref[...] = w_ref[...].astype(BF16)

    o_ref[...] = _dot(a_ref[...], wbf_ref[...]).astype(o_ref.dtype)


def _mm_sigmoid_kernel(a_ref, w_ref, b_ref, o_ref, wbf_ref):
    @pl.when(pl.program_id(1) == 0)
    def _():
        wbf_ref[...] = w_ref[...].astype(BF16)

    acc = _dot(a_ref[...], wbf_ref[...]) + b_ref[...]
    o_ref[...] = jax.nn.sigmoid(acc).astype(o_ref.dtype)


def matmul(a, w, layer, bias=None, *, tm=1024, tn=1024):
    m, k = a.shape
    n = w.shape[2]
    tm = min(tm, m)
    in_specs = [pl.BlockSpec((tm, k), lambda j, i: (i, 0)),
                pl.BlockSpec((None, k, tn), lambda j, i: (layer, 0, j))]
    args = [a, w]
    body = _mm_kernel
    if bias is not None:
        in_specs.append(pl.BlockSpec((1, tn), lambda j, i: (0, j)))
        args.append(bias.reshape(1, n))
        body = _mm_sigmoid_kernel
    return pl.pallas_call(
        body,
        grid=(n // tn, m // tm),
        in_specs=in_specs,
        out_specs=pl.BlockSpec((tm, tn), lambda j, i: (i, j)),
        out_shape=jax.ShapeDtypeStruct((m, n), BF16),
        scratch_shapes=[pltpu.VMEM((k, tn), BF16)],
        compiler_params=_params("arbitrary", "arbitrary"),
        name="matmul_sigmoid" if bias is not None else "matmul",
    )(*args)


def _sb_kernel(q_ref, k_ref, v_ref, o_ref, *, tb, scale):
    qi = pl.program_id(2)
    q = q_ref[...]
    row = lax.broadcasted_iota(jnp.int32, (tb, tb), 0)
    col = lax.broadcasted_iota(jnp.int32, (tb, tb), 1)
    at_or_after = (row >= col).astype(BF16)
    strict = col < row

    def block(kb, carry, diagonal):
        acc, later = carry
        start = pl.multiple_of(kb * tb, tb)
        kblk = k_ref[pl.ds(start, tb), :]
        vblk = v_ref[pl.ds(start, tb), :]
        z = _dot_nt(q, kblk) * scale
        softplus_tail = jnp.log(1.0 + jnp.exp(-jnp.abs(z)))
        log_beta = jnp.minimum(z, 0.0) - softplus_tail
        log_keep = -jnp.maximum(z, 0.0) - softplus_tail
        if diagonal:
            log_keep = jnp.where(strict, log_keep, 0.0)
        hi = log_keep.astype(BF16)
        lo = (log_keep - hi.astype(F32)).astype(BF16)
        suffix_incl = _dot(hi, at_or_after) + _dot(lo, at_or_after)
        log_a = log_beta + (suffix_incl - log_keep) + later
        a = jnp.exp(log_a)
        if diagonal:
            a = jnp.where(strict, a, 0.0)
        acc = acc + _dot(a.astype(BF16), vblk)
        later = later + suffix_incl[:, 0:1]
        return acc, later

    carry = (jnp.zeros((tb, SB_HEAD_DIM), F32), jnp.zeros((tb, 1), F32))
    carry = block(qi, carry, True)
    acc, _ = lax.fori_loop(0, qi, lambda i, c: block(qi - 1 - i, c, False), carry)
    o_ref[...] = acc.astype(o_ref.dtype)


def stick_breaking_attention(proj, batch, seq, *, tb=256):
    nq = seq // tb
    kern = functools.partial(_sb_kernel, tb=tb, scale=SB_HEAD_DIM ** -0.5)
    return pl.pallas_call(
        kern,
        grid=(batch, SB_HEADS, nq),
        in_specs=[
            pl.BlockSpec((tb, SB_HEAD_DIM), lambda b, h, i: (b * nq + i, h)),
            pl.BlockSpec((seq, SB_HEAD_DIM), lambda b, h, i: (b, SB_HEADS + h)),
            pl.BlockSpec((seq, SB_HEAD_DIM), lambda b, h, i: (b, 2 * SB_HEADS + h)),
        ],
        out_specs=pl.BlockSpec((tb, SB_HEAD_DIM), lambda b, h, i: (b * nq + i, h)),
        out_shape=jax.ShapeDtypeStruct((batch * seq, SB_HEADS * SB_HEAD_DIM), BF16),
        compiler_params=_params("arbitrary", "arbitrary", "arbitrary"),
        name="stick_breaking",
    )(proj, proj, proj)


def _gm_kernel(u_ref, v_ref, gv_ref, ws_ref, bt_ref, o_ref, vn_ref, *, nblk):
    v = jax.nn.gelu(v_ref[...].astype(F32))
    vn_ref[...] = (_rms(v) * gv_ref[...]).astype(BF16)
    t = lax.broadcasted_iota(jnp.int32, (GM_BLOCK, GM_BLOCK), 0)
    s = lax.broadcasted_iota(jnp.int32, (GM_BLOCK, GM_BLOCK), 1)
    own_or_earlier = (s // CHUNK) <= (t // CHUNK)
    for g in range(GM_GROUPS):
        cols = slice(g * GM_GROUP_DIM, (g + 1) * GM_GROUP_DIM)
        w = jnp.where(own_or_earlier, ws_ref[g], 0.0).astype(BF16)
        bias = bt_ref[:, g:g + 1]
        for c in range(nblk):
            rows = slice(c * GM_BLOCK, (c + 1) * GM_BLOCK)
            mixed = _dot(w, vn_ref[rows, cols]) + bias
            u = jax.nn.gelu(u_ref[rows, cols].astype(F32))
            o_ref[rows, cols] = (u * mixed).astype(o_ref.dtype)


def spatial_gating(proj, g_vnorm, w_s, b_s, *, tm=512):
    m = proj.shape[0]
    width = GM_GROUPS * GM_GROUP_DIM
    kern = functools.partial(_gm_kernel, nblk=tm // GM_BLOCK)
    return pl.pallas_call(
        kern,
        grid=(m // tm,),
        in_specs=[
            pl.BlockSpec((tm, width), lambda i: (i, 3)),
            pl.BlockSpec((tm, width), lambda i: (i, 4)),
            pl.BlockSpec((1, width), lambda i: (0, 0)),
            pl.BlockSpec((GM_GROUPS, GM_BLOCK, GM_BLOCK), lambda i: (0, 0, 0)),
            pl.BlockSpec((GM_BLOCK, GM_GROUPS), lambda i: (0, 0)),
        ],
        out_specs=pl.BlockSpec((tm, width), lambda i: (i, 0)),
        out_shape=jax.ShapeDtypeStruct((m, width), BF16),
        scratch_shapes=[pltpu.VMEM((tm, width), BF16)],
        compiler_params=_params("arbitrary"),
        name="spatial_gating",
    )(proj, proj, g_vnorm.reshape(1, width), w_s, b_s.T)


def _xa_kernel(q_ref, kv_ref, o_ref, *, scale):
    width = XA_HEADS * XA_HEAD_DIM
    for h in range(XA_HEADS):
        cols = slice(h * XA_HEAD_DIM, (h + 1) * XA_HEAD_DIM)
        vcols = slice(width + h * XA_HEAD_DIM, width + (h + 1) * XA_HEAD_DIM)
        z = _dot_nt(q_ref[:, cols], kv_ref[:, cols]) * scale
        e = jnp.exp(z - jnp.max(z, axis=-1, keepdims=True))
        p = e / jnp.sum(e, axis=-1, keepdims=True)
        o_ref[:, cols] = _dot(p.astype(BF16), kv_ref[:, vcols]).astype(o_ref.dtype)


def memory_cross_attention(proj, mem_kv, batch, seq, *, tq=512):
    width = XA_HEADS * XA_HEAD_DIM
    nq = seq // tq
    kern = functools.partial(_xa_kernel, scale=XA_HEAD_DIM ** -0.5)
    return pl.pallas_call(
        kern,
        grid=(batch, nq),
        in_specs=[
            pl.BlockSpec((tq, width), lambda b, i: (b * nq + i, 5)),
            pl.BlockSpec((N_MEM, 2 * width), lambda b, i: (b, 0)),
        ],
        out_specs=pl.BlockSpec((tq, width), lambda b, i: (b * nq + i, 0)),
        out_shape=jax.ShapeDtypeStruct((batch * seq, width), BF16),
        compiler_params=_params("arbitrary", "arbitrary"),
        name="memory_xattn",
    )(proj, mem_kv)


def _merge_kernel(o0_ref, o1_ref, o2_ref, g0_ref, g1_ref, g2_ref,
                  w0_ref, w1_ref, w2_ref, out_ref, wb0_ref, wb1_ref, wb2_ref):
    @pl.when(pl.program_id(1) == 0)
    def _():
        wb0_ref[...] = w0_ref[...].astype(BF16)
        wb1_ref[...] = w1_ref[...].astype(BF16)
        wb2_ref[...] = w2_ref[...].astype(BF16)

    merged = (g0_ref[...].astype(F32) * _dot(o0_ref[...], wb0_ref[...])
              + g1_ref[...].astype(F32) * _dot(o1_ref[...], wb1_ref[...])
              + g2_ref[...].astype(F32) * _dot(o2_ref[...], wb2_ref[...]))
    out_ref[...] = merged.astype(out_ref.dtype)


def gated_merge(o_sb, o_gm, o_xa, gates, w_sb, w_gm, w_xa, layer, *, tm=1024, tn=512):
    m, k = o_sb.shape
    d = w_sb.shape[2]
    nj = d // tn
    o_spec = pl.BlockSpec((tm, k), lambda j, i: (i, 0))
    w_spec = pl.BlockSpec((None, k, tn), lambda j, i: (layer, 0, j))
    g_specs = [pl.BlockSpec((tm, tn), lambda j, i, br=br: (i, br * nj + j)) for br in range(3)]
    return pl.pallas_call(
        _merge_kernel,
        grid=(nj, m // tm),
        in_specs=[o_spec, o_spec, o_spec, *g_specs, w_spec, w_spec, w_spec],
        out_specs=pl.BlockSpec((tm, tn), lambda j, i: (i, j)),
        out_shape=jax.ShapeDtypeStruct((m, d), BF16),
        scratch_shapes=[pltpu.VMEM((k, tn), BF16)] * 3,
        compiler_params=_params("arbitrary", "arbitrary"),
        name="gated_merge",
    )(o_sb, o_gm, o_xa, gates, gates, gates, w_sb, w_gm, w_xa)


def _mm_residual_kernel(a_ref, w_ref, x_ref, gpost_ref, gnext_ref, xo_ref, ho_ref, *, nk):
    k = pl.program_id(1)
    part = _dot(a_ref[...], w_ref[...])

    @pl.when(k == 0)
    def _():
        xo_ref[...] = part

    @pl.when(k > 0)
    def _():
        xo_ref[...] += part

    @pl.when(k == nk - 1)
    def _():
        xn = x_ref[...] + _rms(xo_ref[...]) * gpost_ref[...]
        xo_ref[...] = xn
        ho_ref[...] = (_rms(xn) * gnext_ref[...]).astype(ho_ref.dtype)


def matmul_norm_residual(a, w_bf16, x, g_post, g_next, *, tm=512, tk=512):
    m, kdim = a.shape
    d = w_bf16.shape[1]
    nk = kdim // tk
    kern = functools.partial(_mm_residual_kernel, nk=nk)
    row = pl.BlockSpec((tm, d), lambda i, k: (i, 0))
    vec = pl.BlockSpec((1, d), lambda i, k: (0, 0))
    return pl.pallas_call(
        kern,
        grid=(m // tm, nk),
        in_specs=[pl.BlockSpec((tm, tk), lambda i, k: (i, k)),
                  pl.BlockSpec((tk, d), lambda i, k: (k, 0)),
                  row, vec, vec],
        out_specs=[row, row],
        out_shape=[jax.ShapeDtypeStruct((m, d), F32), jax.ShapeDtypeStruct((m, d), BF16)],
        compiler_params=_params("arbitrary", "arbitrary"),
        name="matmul_norm_residual",
    )(a, w_bf16, x, g_post.reshape(1, d), g_next.reshape(1, d))


def _ffn_up_kernel(h_ref, wg_ref, wv_ref, cw_ref, cb_ref, o_ref,
                   wgb_ref, wvb_ref, tail_ref, *, blocks_per_seq):
    i = pl.program_id(1)

    @pl.when(i == 0)
    def _():
        wgb_ref[...] = wg_ref[...].astype(BF16)
        wvb_ref[...] = wv_ref[...].astype(BF16)

    @pl.when(i % blocks_per_seq == 0)
    def _():
        tail_ref[...] = jnp.zeros_like(tail_ref)

    h = h_ref[...]
    gate = _dot(h, wgb_ref[...])
    val = _dot(h, wvb_ref[...])
    tm = gate.shape[0]
    w0, w1, w2 = cw_ref[0:1, :], cw_ref[1:2, :], cw_ref[2:3, :]
    bias = cb_ref[...]

    conv = bias + w2 * gate + w1 * pltpu.roll(gate, 1, 0) + w0 * pltpu.roll(gate, 2, 0)
    o_ref[...] = (jax.nn.gelu(conv) * val).astype(o_ref.dtype)

    head = gate[0:F32_SUBLANES, :]
    tail = tail_ref[...]
    r = lax.broadcasted_iota(jnp.int32, head.shape, 0)
    prev1 = jnp.where(r < 1, pltpu.roll(tail, 1, 0), pltpu.roll(head, 1, 0))
    prev2 = jnp.where(r < 2, pltpu.roll(tail, 2, 0), pltpu.roll(head, 2, 0))
    conv_head = bias + w2 * head + w1 * prev1 + w0 * prev2
    o_ref[0:F32_SUBLANES, :] = (jax.nn.gelu(conv_head) * val[0:F32_SUBLANES, :]).astype(o_ref.dtype)

    tail_ref[...] = gate[tm - F32_SUBLANES:tm, :]


def ffn_up(h, w_up, layer, conv_w, conv_b, seq, *, tm=1024, tf=512):
    m, d = h.shape
    d_ff = conv_w.shape[1]
    nf = d_ff // tf
    kern = functools.partial(_ffn_up_kernel, blocks_per_seq=seq // tm)
    return pl.pallas_call(
        kern,
        grid=(nf, m // tm),
        in_specs=[
            pl.BlockSpec((tm, d), lambda j, i: (i, 0)),
            pl.BlockSpec((None, d, tf), lambda j, i: (layer, 0, j)),
            pl.BlockSpec((None, d, tf), lambda j, i: (layer, 0, nf + j)),
            pl.BlockSpec((CONV_W, tf), lambda j, i: (0, j)),
            pl.BlockSpec((1, tf), lambda j, i: (0, j)),
        ],
        out_specs=pl.BlockSpec((tm, tf), lambda j, i: (i, j)),
        out_shape=jax.ShapeDtypeStruct((m, d_ff), BF16),
        scratch_shapes=[pltpu.VMEM((d, tf), BF16), pltpu.VMEM((d, tf), BF16),
                        pltpu.VMEM((F32_SUBLANES, tf), F32)],
        compiler_params=_params("arbitrary", "arbitrary"),
        name="ffn_up",
    )(h, w_up, w_up, conv_w, conv_b.reshape(1, d_ff))


def kernel(x, mem, g_mix_pre, w_in, g_vnorm, w_s, b_s, g_mem, w_mem_kv, w_gate, b_gate,
           w_br_sb, w_br_gm, w_br_xa, w_out, g_mix_post, g_ffn_pre, w_up, conv_w, conv_b,
           w_down, g_ffn_post):
    batch, seq, d = x.shape
    depth = w_in.shape[0]
    xf = x.reshape(batch * seq, d)
    memf = mem.reshape(batch * N_MEM, d)

    h = rmsnorm_bf16(xf, g_mix_pre[0])
    for l in range(depth):
        proj = matmul(h, w_in, l)
        gates = matmul(h, w_gate, l, b_gate[l])
        mem_kv = matmul(rmsnorm_bf16(memf, g_mem[l]), w_mem_kv, l)
        o_sb = stick_breaking_attention(proj, batch, seq)
        o_gm = spatial_gating(proj, g_vnorm[l], w_s[l], b_s[l])
        o_xa = memory_cross_attention(proj, mem_kv, batch, seq)
        merged = gated_merge(o_sb, o_gm, o_xa, gates, w_br_sb, w_br_gm, w_br_xa, l)
        xf, h = matmul_norm_residual(merged, cast_bf16(w_out, l), xf,
                                     g_mix_post[l], g_ffn_pre[l])
        act = ffn_up(h, w_up, l, conv_w[l], conv_b[l], seq)
        g_next = g_mix_pre[l + 1] if l + 1 < depth else g_mix_pre[0]
        xf, h = matmul_norm_residual(act, cast_bf16(w_down, l), xf, g_ffn_post[l], g_next)
    return xf.reshape(batch, seq, d)
```

```python
import functools

import jax
import jax.numpy as jnp
from jax import lax
from jax.experimental import pallas as pl
from jax.experimental.pallas import tpu as pltpu

F32 = jnp.float32
BF16 = jnp.bfloat16

EPS = 1e-6
LOG2_E = 1.4426950408889634
CHUNK = 64
N_MEM = 256
SB_HEADS = 8
SB_HEAD_DIM = 128
GM_GROUPS = 8
GM_GROUP_DIM = 128
GM_BLOCK = 128
XA_HEADS = 4
XA_HEAD_DIM = 256
CONV_W = 3

V7X_VMEM_BYTES = 64 * 1024 * 1024
VMEM_LIMIT_BYTES = V7X_VMEM_BYTES - 8 * 1024 * 1024
F32_SUBLANES = 8


def _params(*semantics):
    return pltpu.CompilerParams(dimension_semantics=semantics,
                                vmem_limit_bytes=VMEM_LIMIT_BYTES)


def _rms(x):
    return x * lax.rsqrt(jnp.mean(x * x, axis=-1, keepdims=True) + EPS)


def _dot(a, b):
    return jnp.dot(a, b, preferred_element_type=F32)


def _dot_nt(a, b):
    return lax.dot_general(a, b, (((1,), (1,)), ((), ())), preferred_element_type=F32)


def _rmsnorm_kernel(x_ref, g_ref, o_ref):
    o_ref[...] = (_rms(x_ref[...]) * g_ref[...]).astype(o_ref.dtype)


def rmsnorm_bf16(x, g, *, tm=512):
    m, d = x.shape
    return pl.pallas_call(
        _rmsnorm_kernel,
        grid=(m // tm,),
        in_specs=[pl.BlockSpec((tm, d), lambda i: (i, 0)),
                  pl.BlockSpec((1, d), lambda i: (0, 0))],
        out_specs=pl.BlockSpec((tm, d), lambda i: (i, 0)),
        out_shape=jax.ShapeDtypeStruct((m, d), BF16),
        compiler_params=_params("arbitrary"),
        name="rmsnorm",
    )(x, g.reshape(1, d))


def _cast_kernel(w_ref, o_ref):
    o_ref[...] = w_ref[...].astype(o_ref.dtype)


def cast_bf16(w, layer, *, tk=512):
    _, k, n = w.shape
    return pl.pallas_call(
        _cast_kernel,
        grid=(k // tk,),
        in_specs=[pl.BlockSpec((None, tk, n), lambda i: (layer, i, 0))],
        out_specs=pl.BlockSpec((tk, n), lambda i: (i, 0)),
        out_shape=jax.ShapeDtypeStruct((k, n), BF16),
        compiler_params=_params("arbitrary"),
        name="cast_bf16",
    )(w)


def _mm_kernel(a_ref, w_ref, o_ref, wbf_ref):
    @pl.when(pl.program_id(1) == 0)
    def _():
        wbf_ref[...] = w_ref[...].astype(BF16)

    o_ref[...] = _dot(a_ref[...], wbf_ref[...]).astype(o_ref.dtype)


def _mm_sigmoid_kernel(a_ref, w_ref, b_ref, o_ref, wbf_ref):
    @pl.when(pl.program_id(1) == 0)
    def _():
        wbf_ref[...] = w_ref[...].astype(BF16)

    acc = _dot(a_ref[...], wbf_ref[...]) + b_ref[...]
    o_ref[...] = jax.nn.sigmoid(acc).astype(o_ref.dtype)


def matmul(a, w, layer, bias=None, *, tm=1024, tn=1024):
    m, k = a.shape
    n = w.shape[2]
    tm = min(tm, m)
    in_specs = [pl.BlockSpec((tm, k), lambda j, i: (i, 0)),
                pl.BlockSpec((None, k, tn), lambda j, i: (layer, 0, j))]
    args = [a, w]
    body = _mm_kernel
    if bias is not None:
        in_specs.append(pl.BlockSpec((1, tn), lambda j, i: (0, j)))
        args.append(bias.reshape(1, n))
        body = _mm_sigmoid_kernel
    return pl.pallas_call(
        body,
        grid=(n // tn, m // tm),
        in_specs=in_specs,
        out_specs=pl.BlockSpec((tm, tn), lambda j, i: (i, j)),
        out_shape=jax.ShapeDtypeStruct((m, n), BF16),
        scratch_shapes=[pltpu.VMEM((k, tn), BF16)],
        compiler_params=_params("arbitrary", "arbitrary"),
        name="matmul_sigmoid" if bias is not None else "matmul",
    )(*args)


def _sb_kernel(q_ref, k_ref, v_ref, o_ref, after_ref, acc_ref, later_ref, *, tb, heads, scale):
    qi = pl.program_id(2)
    j = lax.broadcasted_iota(jnp.int32, (2 * tb, tb), 0)
    s = lax.broadcasted_iota(jnp.int32, (2 * tb, tb), 1)
    after_ref[...] = jnp.where(j % tb > s, -1.0, 0.0).astype(BF16)
    head_cols = [slice(g * SB_HEAD_DIM, (g + 1) * SB_HEAD_DIM) for g in range(heads)]
    exp2_coeff = -scale * LOG2_E

    def block(kb, diagonal):
        start = pl.multiple_of(kb * tb, tb)
        hs = range(heads)
        raw = [_dot_nt(q_ref[:, c], k_ref[pl.ds(start, tb), c]) for c in head_cols]
        zs = [r * scale for r in raw]
        tails = [jnp.log(1.0 + jnp.exp2(jnp.abs(r) * exp2_coeff)) for r in raw]
        log_beta = [jnp.minimum(zs[g], 0.0) - tails[g] for g in hs]
        cost = [jnp.maximum(zs[g], 0.0) + tails[g] for g in hs]
        if diagonal:
            row = lax.broadcasted_iota(jnp.int32, (tb, tb), 0)
            col = lax.broadcasted_iota(jnp.int32, (tb, tb), 1)
            strict = col < row
            cost = [jnp.where(strict, c, 0.0) for c in cost]
        hi = [c.astype(BF16) for c in cost]
        lo = [(cost[g] - hi[g].astype(F32)).astype(BF16) for g in hs]
        suffix = [_dot(jnp.concatenate([hi[g], lo[g]], axis=1), after_ref[...]) for g in hs]
        if diagonal:
            a = [jnp.where(strict, jnp.exp(log_beta[g] + suffix[g]), 0.0) for g in hs]
        else:
            a = [jnp.exp(log_beta[g] + suffix[g] + later_ref[g]) for g in hs]
        for g in hs:
            pv = _dot(a[g].astype(BF16), v_ref[pl.ds(start, tb), head_cols[g]])
            block_sum = suffix[g][:, 0:1] - cost[g][:, 0:1]
            if diagonal:
                acc_ref[g] = pv
                later_ref[g] = block_sum
            else:
                acc_ref[g] += pv
                later_ref[g] += block_sum

    block(qi, True)

    @pl.loop(0, qi)
    def _(i):
        block(qi - 1 - i, False)

    for g, c in enumerate(head_cols):
        o_ref[:, c] = acc_ref[g].astype(o_ref.dtype)


def stick_breaking_attention(proj, batch, seq, *, tb=256, heads=4):
    nq = seq // tb
    groups = SB_HEADS // heads
    width = heads * SB_HEAD_DIM
    kern = functools.partial(_sb_kernel, tb=tb, heads=heads, scale=SB_HEAD_DIM ** -0.5)
    return pl.pallas_call(
        kern,
        grid=(batch, groups, nq),
        in_specs=[
            pl.BlockSpec((tb, width), lambda b, h, i: (b * nq + i, h)),
            pl.BlockSpec((seq, width), lambda b, h, i: (b, groups + h)),
            pl.BlockSpec((seq, width), lambda b, h, i: (b, 2 * groups + h)),
        ],
        out_specs=pl.BlockSpec((tb, width), lambda b, h, i: (b * nq + i, h)),
        out_shape=jax.ShapeDtypeStruct((batch * seq, SB_HEADS * SB_HEAD_DIM), BF16),
        scratch_shapes=[pltpu.VMEM((2 * tb, tb), BF16),
                        pltpu.VMEM((heads, tb, SB_HEAD_DIM), F32),
                        pltpu.VMEM((heads, tb, 1), F32)],
        compiler_params=_params("arbitrary", "arbitrary", "arbitrary"),
        name="stick_breaking",
    )(proj, proj, proj)


def _gm_kernel(u_ref, v_ref, gv_ref, ws_ref, bt_ref, o_ref, vn_ref, *, nblk):
    v = jax.nn.gelu(v_ref[...].astype(F32))
    vn_ref[...] = (_rms(v) * gv_ref[...]).astype(BF16)
    t = lax.broadcasted_iota(jnp.int32, (GM_BLOCK, GM_BLOCK), 0)
    s = lax.broadcasted_iota(jnp.int32, (GM_BLOCK, GM_BLOCK), 1)
    own_or_earlier = (s // CHUNK) <= (t // CHUNK)
    for g in range(GM_GROUPS):
        cols = slice(g * GM_GROUP_DIM, (g + 1) * GM_GROUP_DIM)
        w = jnp.where(own_or_earlier, ws_ref[g], 0.0).astype(BF16)
        bias = bt_ref[:, g:g + 1]
        for c in range(nblk):
            rows = slice(c * GM_BLOCK, (c + 1) * GM_BLOCK)
            mixed = _dot(w, vn_ref[rows, cols]) + bias
            u = jax.nn.gelu(u_ref[rows, cols].astype(F32))
            o_ref[rows, cols] = (u * mixed).astype(o_ref.dtype)


def spatial_gating(proj, g_vnorm, w_s, b_s, *, tm=512):
    m = proj.shape[0]
    width = GM_GROUPS * GM_GROUP_DIM
    kern = functools.partial(_gm_kernel, nblk=tm // GM_BLOCK)
    return pl.pallas_call(
        kern,
        grid=(m // tm,),
        in_specs=[
            pl.BlockSpec((tm, width), lambda i: (i, 3)),
            pl.BlockSpec((tm, width), lambda i: (i, 4)),
            pl.BlockSpec((1, width), lambda i: (0, 0)),
            pl.BlockSpec((GM_GROUPS, GM_BLOCK, GM_BLOCK), lambda i: (0, 0, 0)),
            pl.BlockSpec((GM_BLOCK, GM_GROUPS), lambda i: (0, 0)),
        ],
        out_specs=pl.BlockSpec((tm, width), lambda i: (i, 0)),
        out_shape=jax.ShapeDtypeStruct((m, width), BF16),
        scratch_shapes=[pltpu.VMEM((tm, width), BF16)],
        compiler_params=_params("arbitrary"),
        name="spatial_gating",
    )(proj, proj, g_vnorm.reshape(1, width), w_s, b_s.T)


def _xa_kernel(q_ref, kv_ref, o_ref, *, scale):
    width = XA_HEADS * XA_HEAD_DIM
    for h in range(XA_HEADS):
        cols = slice(h * XA_HEAD_DIM, (h + 1) * XA_HEAD_DIM)
        vcols = slice(width + h * XA_HEAD_DIM, width + (h + 1) * XA_HEAD_DIM)
        z = _dot_nt(q_ref[:, cols], kv_ref[:, cols]) * scale
        e = jnp.exp(z - jnp.max(z, axis=-1, keepdims=True))
        p = e / jnp.sum(e, axis=-1, keepdims=True)
        o_ref[:, cols] = _dot(p.astype(BF16), kv_ref[:, vcols]).astype(o_ref.dtype)


def memory_cross_attention(proj, mem_kv, batch, seq, *, tq=512):
    width = XA_HEADS * XA_HEAD_DIM
    nq = seq // tq
    kern = functools.partial(_xa_kernel, scale=XA_HEAD_DIM ** -0.5)
    return pl.pallas_call(
        kern,
        grid=(batch, nq),
        in_specs=[
            pl.BlockSpec((tq, width), lambda b, i: (b * nq + i, 5)),
            pl.BlockSpec((N_MEM, 2 * width), lambda b, i: (b, 0)),
        ],
        out_specs=pl.BlockSpec((tq, width), lambda b, i: (b * nq + i, 0)),
        out_shape=jax.ShapeDtypeStruct((batch * seq, width), BF16),
        compiler_params=_params("arbitrary", "arbitrary"),
        name="memory_xattn",
    )(proj, mem_kv)


def _merge_kernel(o0_ref, o1_ref, o2_ref, g0_ref, g1_ref, g2_ref,
                  w0_ref, w1_ref, w2_ref, out_ref, wb0_ref, wb1_ref, wb2_ref):
    @pl.when(pl.program_id(1) == 0)
    def _():
        wb0_ref[...] = w0_ref[...].astype(BF16)
        wb1_ref[...] = w1_ref[...].astype(BF16)
        wb2_ref[...] = w2_ref[...].astype(BF16)

    merged = (g0_ref[...].astype(F32) * _dot(o0_ref[...], wb0_ref[...])
              + g1_ref[...].astype(F32) * _dot(o1_ref[...], wb1_ref[...])
              + g2_ref[...].astype(F32) * _dot(o2_ref[...], wb2_ref[...]))
    out_ref[...] = merged.astype(out_ref.dtype)


def gated_merge(o_sb, o_gm, o_xa, gates, w_sb, w_gm, w_xa, layer, *, tm=1024, tn=512):
    m, k = o_sb.shape
    d = w_sb.shape[2]
    nj = d // tn
    o_spec = pl.BlockSpec((tm, k), lambda j, i: (i, 0))
    w_spec = pl.BlockSpec((None, k, tn), lambda j, i: (layer, 0, j))
    g_specs = [pl.BlockSpec((tm, tn), lambda j, i, br=br: (i, br * nj + j)) for br in range(3)]
    return pl.pallas_call(
        _merge_kernel,
        grid=(nj, m // tm),
        in_specs=[o_spec, o_spec, o_spec, *g_specs, w_spec, w_spec, w_spec],
        out_specs=pl.BlockSpec((tm, tn), lambda j, i: (i, j)),
        out_shape=jax.ShapeDtypeStruct((m, d), BF16),
        scratch_shapes=[pltpu.VMEM((k, tn), BF16)] * 3,
        compiler_params=_params("arbitrary", "arbitrary"),
        name="gated_merge",
    )(o_sb, o_gm, o_xa, gates, gates, gates, w_sb, w_gm, w_xa)


def _mm_residual_kernel(a_ref, w_ref, x_ref, gpost_ref, *rest, nk):
    k = pl.program_id(1)
    xo_ref = rest[-2] if len(rest) == 3 else rest[0]

    def finish(y):
        xn = x_ref[...] + _rms(y) * gpost_ref[...]
        xo_ref[...] = xn
        if len(rest) == 3:
            gnext_ref, _, ho_ref = rest
            ho_ref[...] = (_rms(xn) * gnext_ref[...]).astype(ho_ref.dtype)

    if nk == 1:
        finish(_dot(a_ref[...], w_ref[...]))
        return

    @pl.when(k == 0)
    def _():
        xo_ref[...] = jnp.zeros_like(xo_ref)

    xo_ref[...] += _dot(a_ref[...], w_ref[...])

    @pl.when(k == nk - 1)
    def _():
        finish(xo_ref[...])


def matmul_norm_residual(a, w_bf16, x, g_post, g_next=None, *, tm=512, nk=1):
    m, kdim = a.shape
    d = w_bf16.shape[1]
    tk = kdim // nk
    kern = functools.partial(_mm_residual_kernel, nk=nk)
    row = pl.BlockSpec((tm, d), lambda i, k: (i, 0))
    vec = pl.BlockSpec((1, d), lambda i, k: (0, 0))
    in_specs = [pl.BlockSpec((tm, tk), lambda i, k: (i, k)),
                pl.BlockSpec((tk, d), lambda i, k: (k, 0)),
                row, vec]
    args = [a, w_bf16, x, g_post.reshape(1, d)]
    out_specs = [row]
    out_shape = [jax.ShapeDtypeStruct((m, d), F32)]
    if g_next is not None:
        in_specs.append(vec)
        args.append(g_next.reshape(1, d))
        out_specs.append(row)
        out_shape.append(jax.ShapeDtypeStruct((m, d), BF16))
    outs = pl.pallas_call(
        kern,
        grid=(m // tm, nk),
        in_specs=in_specs,
        out_specs=out_specs,
        out_shape=out_shape,
        compiler_params=_params("arbitrary", "arbitrary"),
        name="matmul_norm_residual",
    )(*args)
    return (outs[0], outs[1]) if g_next is not None else (outs[0], None)


def _ffn_up_kernel(h_ref, wg_ref, wv_ref, cw_ref, cb_ref, o_ref,
                   wgb_ref, wvb_ref, tail_ref, *, blocks_per_seq):
    i = pl.program_id(1)

    @pl.when(i == 0)
    def _():
        wgb_ref[...] = wg_ref[...].astype(BF16)
        wvb_ref[...] = wv_ref[...].astype(BF16)

    @pl.when(i % blocks_per_seq == 0)
    def _():
        tail_ref[...] = jnp.zeros_like(tail_ref)

    h = h_ref[...]
    gate = _dot(h, wgb_ref[...])
    val = _dot(h, wvb_ref[...])
    tm = gate.shape[0]
    w0, w1, w2 = cw_ref[0:1, :], cw_ref[1:2, :], cw_ref[2:3, :]
    bias = cb_ref[...]

    conv = bias + w2 * gate + w1 * pltpu.roll(gate, 1, 0) + w0 * pltpu.roll(gate, 2, 0)
    o_ref[...] = (jax.nn.gelu(conv) * val).astype(o_ref.dtype)

    head = gate[0:F32_SUBLANES, :]
    tail = tail_ref[...]
    r = lax.broadcasted_iota(jnp.int32, head.shape, 0)
    prev1 = jnp.where(r < 1, pltpu.roll(tail, 1, 0), pltpu.roll(head, 1, 0))
    prev2 = jnp.where(r < 2, pltpu.roll(tail, 2, 0), pltpu.roll(head, 2, 0))
    conv_head = bias + w2 * head + w1 * prev1 + w0 * prev2
    o_ref[0:F32_SUBLANES, :] = (jax.nn.gelu(conv_head) * val[0:F32_SUBLANES, :]).astype(o_ref.dtype)

    tail_ref[...] = gate[tm - F32_SUBLANES:tm, :]


def ffn_up(h, w_up, layer, conv_w, conv_b, seq, *, tm=1024, tf=512):
    m, d = h.shape
    d_ff = conv_w.shape[1]
    nf = d_ff // tf
    kern = functools.partial(_ffn_up_kernel, blocks_per_seq=seq // tm)
    return pl.pallas_call(
        kern,
        grid=(nf, m // tm),
        in_specs=[
            pl.BlockSpec((tm, d), lambda j, i: (i, 0)),
            pl.BlockSpec((None, d, tf), lambda j, i: (layer, 0, j)),
            pl.BlockSpec((None, d, tf), lambda j, i: (layer, 0, nf + j)),
            pl.BlockSpec((CONV_W, tf), lambda j, i: (0, j)),
            pl.BlockSpec((1, tf), lambda j, i: (0, j)),
        ],
        out_specs=pl.BlockSpec((tm, tf), lambda j, i: (i, j)),
        out_shape=jax.ShapeDtypeStruct((m, d_ff), BF16),
        scratch_shapes=[pltpu.VMEM((d, tf), BF16), pltpu.VMEM((d, tf), BF16),
                        pltpu.VMEM((F32_SUBLANES, tf), F32)],
        compiler_params=_params("arbitrary", "arbitrary"),
        name="ffn_up",
    )(h, w_up, w_up, conv_w, conv_b.reshape(1, d_ff))


def kernel(x, mem, g_mix_pre, w_in, g_vnorm, w_s, b_s, g_mem, w_mem_kv, w_gate, b_gate,
           w_br_sb, w_br_gm, w_br_xa, w_out, g_mix_post, g_ffn_pre, w_up, conv_w, conv_b,
           w_down, g_ffn_post):
    batch, seq, d = x.shape
    depth = w_in.shape[0]
    xf = x.reshape(batch * seq, d)
    memf = mem.reshape(batch * N_MEM, d)

    h = rmsnorm_bf16(xf, g_mix_pre[0])
    for l in range(depth):
        proj = matmul(h, w_in, l)
        gates = matmul(h, w_gate, l, b_gate[l])
        mem_kv = matmul(rmsnorm_bf16(memf, g_mem[l]), w_mem_kv, l)
        o_sb = stick_breaking_attention(proj, batch, seq)
        o_gm = spatial_gating(proj, g_vnorm[l], w_s[l], b_s[l])
        o_xa = memory_cross_attention(proj, mem_kv, batch, seq)
        merged = gated_merge(o_sb, o_gm, o_xa, gates, w_br_sb, w_br_gm, w_br_xa, l)
        xf, h = matmul_norm_residual(merged, cast_bf16(w_out, l), xf,
                                     g_mix_post[l], g_ffn_pre[l])
        act = ffn_up(h, w_up, l, conv_w[l], conv_b[l], seq)
        g_next = g_mix_pre[l + 1] if l + 1 < depth else None
        xf, h = matmul_norm_residual(act, cast_bf16(w_down, l), xf, g_ffn_post[l], g_next, nk=4)
    return xf.reshape(batch, seq, d)
```

```python
import functools

import jax
import jax.numpy as jnp
from jax import lax
from jax.experimental import pallas as pl
from jax.experimental.pallas import tpu as pltpu

F32 = jnp.float32
BF16 = jnp.bfloat16

EPS = 1e-6
LOG2_E = 1.4426950408889634
F32_EXP_UNDERFLOW_LOG = -104.0
CHUNK = 64
N_MEM = 256
SB_HEADS = 8
SB_HEAD_DIM = 128
GM_GROUPS = 8
GM_GROUP_DIM = 128
GM_BLOCK = 128
XA_HEADS = 4
XA_HEAD_DIM = 256
CONV_W = 3

V7X_VMEM_BYTES = 64 * 1024 * 1024
VMEM_LIMIT_BYTES = V7X_VMEM_BYTES - 8 * 1024 * 1024
F32_SUBLANES = 8


def _params(*semantics):
    return pltpu.CompilerParams(dimension_semantics=semantics,
                                vmem_limit_bytes=VMEM_LIMIT_BYTES)


def _rms(x):
    return x * lax.rsqrt(jnp.mean(x * x, axis=-1, keepdims=True) + EPS)


def _dot(a, b):
    return jnp.dot(a, b, preferred_element_type=F32)


def _dot_nt(a, b):
    return lax.dot_general(a, b, (((1,), (1,)), ((), ())), preferred_element_type=F32)


def _rmsnorm_kernel(x_ref, g_ref, o_ref):
    o_ref[...] = (_rms(x_ref[...]) * g_ref[...]).astype(o_ref.dtype)


def rmsnorm_bf16(x, g, *, tm=512):
    m, d = x.shape
    return pl.pallas_call(
        _rmsnorm_kernel,
        grid=(m // tm,),
        in_specs=[pl.BlockSpec((tm, d), lambda i: (i, 0)),
                  pl.BlockSpec((1, d), lambda i: (0, 0))],
        out_specs=pl.BlockSpec((tm, d), lambda i: (i, 0)),
        out_shape=jax.ShapeDtypeStruct((m, d), BF16),
        compiler_params=_params("arbitrary"),
        name="rmsnorm",
    )(x, g.reshape(1, d))


def _cast_kernel(w_ref, o_ref):
    o_ref[...] = w_ref[...].astype(o_ref.dtype)


def cast_bf16(w, layer, *, tk=512):
    _, k, n = w.shape
    return pl.pallas_call(
        _cast_kernel,
        grid=(k // tk,),
        in_specs=[pl.BlockSpec((None, tk, n), lambda i: (layer, i, 0))],
        out_specs=pl.BlockSpec((tk, n), lambda i: (i, 0)),
        out_shape=jax.ShapeDtypeStruct((k, n), BF16),
        compiler_params=_params("arbitrary"),
        name="cast_bf16",
    )(w)


def _mm_kernel(a_ref, w_ref, o_ref, wbf_ref):
    @pl.when(pl.program_id(1) == 0)
    def _():
        wbf_ref[...] = w_ref[...].astype(BF16)

    o_ref[...] = _dot(a_ref[...], wbf_ref[...]).astype(o_ref.dtype)


def _mm_sigmoid_kernel(a_ref, w_ref, b_ref, o_ref, wbf_ref):
    @pl.when(pl.program_id(1) == 0)
    def _():
        wbf_ref[...] = w_ref[...].astype(BF16)

    acc = _dot(a_ref[...], wbf_ref[...]) + b_ref[...]
    o_ref[...] = (0.5 * jnp.tanh(0.5 * acc) + 0.5).astype(o_ref.dtype)


def matmul(a, w, layer, bias=None, *, tm=1024, tn=1024):
    m, k = a.shape
    n = w.shape[2]
    tm = min(tm, m)
    in_specs = [pl.BlockSpec((tm, k), lambda j, i: (i, 0)),
                pl.BlockSpec((None, k, tn), lambda j, i: (layer, 0, j))]
    args = [a, w]
    body = _mm_kernel
    if bias is not None:
        in_specs.append(pl.BlockSpec((1, tn), lambda j, i: (0, j)))
        args.append(bias.reshape(1, n))
        body = _mm_sigmoid_kernel
    return pl.pallas_call(
        body,
        grid=(n // tn, m // tm),
        in_specs=in_specs,
        out_specs=pl.BlockSpec((tm, tn), lambda j, i: (i, j)),
        out_shape=jax.ShapeDtypeStruct((m, n), BF16),
        scratch_shapes=[pltpu.VMEM((k, tn), BF16)],
        compiler_params=_params("arbitrary", "arbitrary"),
        name="matmul_sigmoid" if bias is not None else "matmul",
    )(*args)


def _sb_kernel(q_ref, k_ref, v_ref, o_ref, after_ref, acc_ref, later_ref, *, tb, heads, scale):
    qi = pl.program_id(2)
    j = lax.broadcasted_iota(jnp.int32, (2 * tb, tb), 0)
    s = lax.broadcasted_iota(jnp.int32, (2 * tb, tb), 1)
    after_ref[...] = jnp.where(j % tb > s, -1.0, 0.0).astype(BF16)
    head_cols = [slice(g * SB_HEAD_DIM, (g + 1) * SB_HEAD_DIM) for g in range(heads)]
    exp2_coeff = -scale * LOG2_E

    def block(kb, diagonal):
        start = pl.multiple_of(kb * tb, tb)
        hs = range(heads)
        raw = [_dot_nt(q_ref[:, c], k_ref[pl.ds(start, tb), c]) for c in head_cols]
        zs = [r * scale for r in raw]
        tails = [jnp.log(1.0 + jnp.exp2(jnp.abs(r) * exp2_coeff)) for r in raw]
        log_beta = [jnp.minimum(zs[g], 0.0) - tails[g] for g in hs]
        cost = [jnp.maximum(zs[g], 0.0) + tails[g] for g in hs]
        if diagonal:
            row = lax.broadcasted_iota(jnp.int32, (tb, tb), 0)
            col = lax.broadcasted_iota(jnp.int32, (tb, tb), 1)
            strict = col < row
            cost = [jnp.where(strict, c, 0.0) for c in cost]
        hi = [c.astype(BF16) for c in cost]
        lo = [(cost[g] - hi[g].astype(F32)).astype(BF16) for g in hs]
        suffix = [_dot(jnp.concatenate([hi[g], lo[g]], axis=1), after_ref[...]) for g in hs]
        if diagonal:
            a = [jnp.where(strict, jnp.exp(log_beta[g] + suffix[g]), 0.0) for g in hs]
        else:
            a = [jnp.exp(log_beta[g] + suffix[g] + later_ref[g]) for g in hs]
        for g in hs:
            pv = _dot(a[g].astype(BF16), v_ref[pl.ds(start, tb), head_cols[g]])
            block_sum = suffix[g][:, 0:1] - cost[g][:, 0:1]
            if diagonal:
                acc_ref[g] = pv
                later_ref[g] = block_sum
            else:
                acc_ref[g] += pv
                later_ref[g] += block_sum

    block(qi, True)

    def more_blocks(state):
        i, reachable = state
        return jnp.logical_and(i < qi, reachable)

    def next_block(state):
        i, _ = state
        block(qi - 1 - i, False)
        return i + 1, jnp.max(later_ref[...]) > F32_EXP_UNDERFLOW_LOG

    lax.while_loop(more_blocks, next_block,
                   (jnp.int32(0), jnp.max(later_ref[...]) > F32_EXP_UNDERFLOW_LOG))

    for g, c in enumerate(head_cols):
        o_ref[:, c] = acc_ref[g].astype(o_ref.dtype)


def stick_breaking_attention(proj, batch, seq, *, tb=256, heads=8):
    nq = seq // tb
    groups = SB_HEADS // heads
    width = heads * SB_HEAD_DIM
    kern = functools.partial(_sb_kernel, tb=tb, heads=heads, scale=SB_HEAD_DIM ** -0.5)
    return pl.pallas_call(
        kern,
        grid=(batch, groups, nq),
        in_specs=[
            pl.BlockSpec((tb, width), lambda b, h, i: (b * nq + i, h)),
            pl.BlockSpec((seq, width), lambda b, h, i: (b, groups + h)),
            pl.BlockSpec((seq, width), lambda b, h, i: (b, 2 * groups + h)),
        ],
        out_specs=pl.BlockSpec((tb, width), lambda b, h, i: (b * nq + i, h)),
        out_shape=jax.ShapeDtypeStruct((batch * seq, SB_HEADS * SB_HEAD_DIM), BF16),
        scratch_shapes=[pltpu.VMEM((2 * tb, tb), BF16),
                        pltpu.VMEM((heads, tb, SB_HEAD_DIM), F32),
                        pltpu.VMEM((heads, tb, 1), F32)],
        compiler_params=_params("arbitrary", "arbitrary", "arbitrary"),
        name="stick_breaking",
    )(proj, proj, proj)


def _gm_kernel(u_ref, v_ref, gv_ref, ws_ref, bt_ref, o_ref, vn_ref, *, nblk):
    v = jax.nn.gelu(v_ref[...].astype(F32))
    vn_ref[...] = (_rms(v) * gv_ref[...]).astype(BF16)
    t = lax.broadcasted_iota(jnp.int32, (GM_BLOCK, GM_BLOCK), 0)
    s = lax.broadcasted_iota(jnp.int32, (GM_BLOCK, GM_BLOCK), 1)
    own_or_earlier = (s // CHUNK) <= (t // CHUNK)
    for g in range(GM_GROUPS):
        cols = slice(g * GM_GROUP_DIM, (g + 1) * GM_GROUP_DIM)
        w = jnp.where(own_or_earlier, ws_ref[g], 0.0).astype(BF16)
        bias = bt_ref[:, g:g + 1]
        for c in range(nblk):
            rows = slice(c * GM_BLOCK, (c + 1) * GM_BLOCK)
            mixed = _dot(w, vn_ref[rows, cols]) + bias
            u = jax.nn.gelu(u_ref[rows, cols].astype(F32))
            o_ref[rows, cols] = (u * mixed).astype(o_ref.dtype)


def spatial_gating(proj, g_vnorm, w_s, b_s, *, tm=512):
    m = proj.shape[0]
    width = GM_GROUPS * GM_GROUP_DIM
    kern = functools.partial(_gm_kernel, nblk=tm // GM_BLOCK)
    return pl.pallas_call(
        kern,
        grid=(m // tm,),
        in_specs=[
            pl.BlockSpec((tm, width), lambda i: (i, 3)),
            pl.BlockSpec((tm, width), lambda i: (i, 4)),
            pl.BlockSpec((1, width), lambda i: (0, 0)),
            pl.BlockSpec((GM_GROUPS, GM_BLOCK, GM_BLOCK), lambda i: (0, 0, 0)),
            pl.BlockSpec((GM_BLOCK, GM_GROUPS), lambda i: (0, 0)),
        ],
        out_specs=pl.BlockSpec((tm, width), lambda i: (i, 0)),
        out_shape=jax.ShapeDtypeStruct((m, width), BF16),
        scratch_shapes=[pltpu.VMEM((tm, width), BF16)],
        compiler_params=_params("arbitrary"),
        name="spatial_gating",
    )(proj, proj, g_vnorm.reshape(1, width), w_s, b_s.T)


def _xa_kernel(q_ref, kv_ref, o_ref, *, scale):
    width = XA_HEADS * XA_HEAD_DIM
    for h in range(XA_HEADS):
        cols = slice(h * XA_HEAD_DIM, (h + 1) * XA_HEAD_DIM)
        vcols = slice(width + h * XA_HEAD_DIM, width + (h + 1) * XA_HEAD_DIM)
        z = _dot_nt(q_ref[:, cols], kv_ref[:, cols]) * scale
        e = jnp.exp(z - jnp.max(z, axis=-1, keepdims=True))
        p = e / jnp.sum(e, axis=-1, keepdims=True)
        o_ref[:, cols] = _dot(p.astype(BF16), kv_ref[:, vcols]).astype(o_ref.dtype)


def memory_cross_attention(proj, mem_kv, batch, seq, *, tq=512):
    width = XA_HEADS * XA_HEAD_DIM
    nq = seq // tq
    kern = functools.partial(_xa_kernel, scale=XA_HEAD_DIM ** -0.5)
    return pl.pallas_call(
        kern,
        grid=(batch, nq),
        in_specs=[
            pl.BlockSpec((tq, width), lambda b, i: (b * nq + i, 5)),
            pl.BlockSpec((N_MEM, 2 * width), lambda b, i: (b, 0)),
        ],
        out_specs=pl.BlockSpec((tq, width), lambda b, i: (b * nq + i, 0)),
        out_shape=jax.ShapeDtypeStruct((batch * seq, width), BF16),
        compiler_params=_params("arbitrary", "arbitrary"),
        name="memory_xattn",
    )(proj, mem_kv)


def _merge_kernel(o0_ref, o1_ref, o2_ref, g0_ref, g1_ref, g2_ref,
                  w0_ref, w1_ref, w2_ref, out_ref, wb0_ref, wb1_ref, wb2_ref):
    @pl.when(pl.program_id(1) == 0)
    def _():
        wb0_ref[...] = w0_ref[...].astype(BF16)
        wb1_ref[...] = w1_ref[...].astype(BF16)
        wb2_ref[...] = w2_ref[...].astype(BF16)

    merged = (g0_ref[...].astype(F32) * _dot(o0_ref[...], wb0_ref[...])
              + g1_ref[...].astype(F32) * _dot(o1_ref[...], wb1_ref[...])
              + g2_ref[...].astype(F32) * _dot(o2_ref[...], wb2_ref[...]))
    out_ref[...] = merged.astype(out_ref.dtype)


def gated_merge(o_sb, o_gm, o_xa, gates, w_sb, w_gm, w_xa, layer, *, tm=1024, tn=512):
    m, k = o_sb.shape
    d = w_sb.shape[2]
    nj = d // tn
    o_spec = pl.BlockSpec((tm, k), lambda j, i: (i, 0))
    w_spec = pl.BlockSpec((None, k, tn), lambda j, i: (layer, 0, j))
    g_specs = [pl.BlockSpec((tm, tn), lambda j, i, br=br: (i, br * nj + j)) for br in range(3)]
    return pl.pallas_call(
        _merge_kernel,
        grid=(nj, m // tm),
        in_specs=[o_spec, o_spec, o_spec, *g_specs, w_spec, w_spec, w_spec],
        out_specs=pl.BlockSpec((tm, tn), lambda j, i: (i, j)),
        out_shape=jax.ShapeDtypeStruct((m, d), BF16),
        scratch_shapes=[pltpu.VMEM((k, tn), BF16)] * 3,
        compiler_params=_params("arbitrary", "arbitrary"),
        name="gated_merge",
    )(o_sb, o_gm, o_xa, gates, gates, gates, w_sb, w_gm, w_xa)


def _mm_residual_kernel(a_ref, w_ref, x_ref, gpost_ref, *rest, nk):
    k = pl.program_id(1)
    xo_ref = rest[-2] if len(rest) == 3 else rest[0]

    def finish(y):
        xn = x_ref[...] + _rms(y) * gpost_ref[...]
        xo_ref[...] = xn
        if len(rest) == 3:
            gnext_ref, _, ho_ref = rest
            ho_ref[...] = (_rms(xn) * gnext_ref[...]).astype(ho_ref.dtype)

    if nk == 1:
        finish(_dot(a_ref[...], w_ref[...]))
        return

    @pl.when(k == 0)
    def _():
        xo_ref[...] = jnp.zeros_like(xo_ref)

    xo_ref[...] += _dot(a_ref[...], w_ref[...])

    @pl.when(k == nk - 1)
    def _():
        finish(xo_ref[...])


def matmul_norm_residual(a, w_bf16, x, g_post, g_next=None, *, tm=512, nk=1):
    m, kdim = a.shape
    d = w_bf16.shape[1]
    tk = kdim // nk
    kern = functools.partial(_mm_residual_kernel, nk=nk)
    row = pl.BlockSpec((tm, d), lambda i, k: (i, 0))
    vec = pl.BlockSpec((1, d), lambda i, k: (0, 0))
    in_specs = [pl.BlockSpec((tm, tk), lambda i, k: (i, k)),
                pl.BlockSpec((tk, d), lambda i, k: (k, 0)),
                row, vec]
    args = [a, w_bf16, x, g_post.reshape(1, d)]
    out_specs = [row]
    out_shape = [jax.ShapeDtypeStruct((m, d), F32)]
    if g_next is not None:
        in_specs.append(vec)
        args.append(g_next.reshape(1, d))
        out_specs.append(row)
        out_shape.append(jax.ShapeDtypeStruct((m, d), BF16))
    outs = pl.pallas_call(
        kern,
        grid=(m // tm, nk),
        in_specs=in_specs,
        out_specs=out_specs,
        out_shape=out_shape,
        compiler_params=_params("arbitrary", "arbitrary"),
        name="matmul_norm_residual",
    )(*args)
    return (outs[0], outs[1]) if g_next is not None else (outs[0], None)


def _ffn_up_kernel(h_ref, wg_ref, wv_ref, cw_ref, cb_ref, o_ref,
                   wgb_ref, wvb_ref, tail_ref, *, blocks_per_seq):
    i = pl.program_id(1)

    @pl.when(i == 0)
    def _():
        wgb_ref[...] = wg_ref[...].astype(BF16)
        wvb_ref[...] = wv_ref[...].astype(BF16)

    @pl.when(i % blocks_per_seq == 0)
    def _():
        tail_ref[...] = jnp.zeros_like(tail_ref)

    h = h_ref[...]
    gate = _dot(h, wgb_ref[...])
    val = _dot(h, wvb_ref[...])
    tm = gate.shape[0]
    w0, w1, w2 = cw_ref[0:1, :], cw_ref[1:2, :], cw_ref[2:3, :]
    bias = cb_ref[...]

    conv = bias + w2 * gate + w1 * pltpu.roll(gate, 1, 0) + w0 * pltpu.roll(gate, 2, 0)
    o_ref[...] = (jax.nn.gelu(conv) * val).astype(o_ref.dtype)

    head = gate[0:F32_SUBLANES, :]
    tail = tail_ref[...]
    r = lax.broadcasted_iota(jnp.int32, head.shape, 0)
    prev1 = jnp.where(r < 1, pltpu.roll(tail, 1, 0), pltpu.roll(head, 1, 0))
    prev2 = jnp.where(r < 2, pltpu.roll(tail, 2, 0), pltpu.roll(head, 2, 0))
    conv_head = bias + w2 * head + w1 * prev1 + w0 * prev2
    o_ref[0:F32_SUBLANES, :] = (jax.nn.gelu(conv_head) * val[0:F32_SUBLANES, :]).astype(o_ref.dtype)

    tail_ref[...] = gate[tm - F32_SUBLANES:tm, :]


def ffn_up(h, w_up, layer, conv_w, conv_b, seq, *, tm=1024, tf=512):
    m, d = h.shape
    d_ff = conv_w.shape[1]
    nf = d_ff // tf
    kern = functools.partial(_ffn_up_kernel, blocks_per_seq=seq // tm)
    return pl.pallas_call(
        kern,
        grid=(nf, m // tm),
        in_specs=[
            pl.BlockSpec((tm, d), lambda j, i: (i, 0)),
            pl.BlockSpec((None, d, tf), lambda j, i: (layer, 0, j)),
            pl.BlockSpec((None, d, tf), lambda j, i: (layer, 0, nf + j)),
            pl.BlockSpec((CONV_W, tf), lambda j, i: (0, j)),
            pl.BlockSpec((1, tf), lambda j, i: (0, j)),
        ],
        out_specs=pl.BlockSpec((tm, tf), lambda j, i: (i, j)),
        out_shape=jax.ShapeDtypeStruct((m, d_ff), BF16),
        scratch_shapes=[pltpu.VMEM((d, tf), BF16), pltpu.VMEM((d, tf), BF16),
                        pltpu.VMEM((F32_SUBLANES, tf), F32)],
        compiler_params=_params("arbitrary", "arbitrary"),
        name="ffn_up",
    )(h, w_up, w_up, conv_w, conv_b.reshape(1, d_ff))


def kernel(x, mem, g_mix_pre, w_in, g_vnorm, w_s, b_s, g_mem, w_mem_kv, w_gate, b_gate,
           w_br_sb, w_br_gm, w_br_xa, w_out, g_mix_post, g_ffn_pre, w_up, conv_w, conv_b,
           w_down, g_ffn_post):
    batch, seq, d = x.shape
    depth = w_in.shape[0]
    xf = x.reshape(batch * seq, d)
    memf = mem.reshape(batch * N_MEM, d)

    h = rmsnorm_bf16(xf, g_mix_pre[0])
    for l in range(depth):
        proj = matmul(h, w_in, l)
        gates = matmul(h, w_gate, l, b_gate[l])
        mem_kv = matmul(rmsnorm_bf16(memf, g_mem[l]), w_mem_kv, l)
        o_sb = stick_breaking_attention(proj, batch, seq)
        o_gm = spatial_gating(proj, g_vnorm[l], w_s[l], b_s[l])
        o_xa = memory_cross_attention(proj, mem_kv, batch, seq)
        merged = gated_merge(o_sb, o_gm, o_xa, gates, w_br_sb, w_br_gm, w_br_xa, l)
        xf, h = matmul_norm_residual(merged, cast_bf16(w_out, l), xf,
                                     g_mix_post[l], g_ffn_pre[l])
        act = ffn_up(h, w_up, l, conv_w[l], conv_b[l], seq)
        g_next = g_mix_pre[l + 1] if l + 1 < depth else None
        xf, h = matmul_norm_residual(act, cast_bf16(w_down, l), xf, g_ffn_post[l], g_next, nk=4)
    return xf.reshape(batch, seq, d)
```
